```python
import math
import jax
import jax.numpy as jnp
from jax import lax
import numpy as np

D_MODEL = 2048
BATCH = 1
SEQ = 16384
DEPTH = 2

F32 = jnp.float32
EPS = 1e-6
MIX_WIDTH = D_MODEL // 2
N_BRANCH = 4

POOL_WINDOWS = (2, 4, 8, 16)
POOL_WIDTH = MIX_WIDTH
POOL_GROUP = POOL_WIDTH // len(POOL_WINDOWS)

SSD_WIDTH = MIX_WIDTH
SSD_HEADDIM = 64
SSD_HEADS = SSD_WIDTH // SSD_HEADDIM
SSD_STATE = 128
SSD_GROUPS = 2
SSD_CONV = 4
SSD_CHUNK = 128
SSD_CONV_DIM = SSD_WIDTH + 2 * SSD_GROUPS * SSD_STATE
DT_MIN = 1e-3
DT_MAX = 1e-1

MLA_HEADS = 8
MLA_NOPE = 128
MLA_ROPE = 64
MLA_QK = MLA_NOPE + MLA_ROPE
MLA_V = MIX_WIDTH // MLA_HEADS
MLA_Q_LORA = 768
MLA_KV_LORA = 512
ROPE_THETA = 10000.0
ATT_BLOCK = 128
MAX_POS_OFFSET = 4096

SGU_WIDTH = MIX_WIDTH
SGU_CHUNK = 128
SGU_GROUPS = 8
SGU_GROUP_CH = SGU_WIDTH // SGU_GROUPS

IN_SPLITS = (POOL_WIDTH, SSD_WIDTH, SSD_CONV_DIM, SSD_HEADS, MLA_Q_LORA, MLA_KV_LORA, MLA_ROPE, 2 * SGU_WIDTH, N_BRANCH * D_MODEL)
D_IN = sum(IN_SPLITS)

D_FF = 5632
N_EXPERTS = 8
TOP_K = 2
D_FF_EXPERT = 7168
MOE_BLOCK = 512
N_DENSE = (DEPTH + 1) // 2
N_MOE = DEPTH // 2

kernel_name = 'hybrid_parallel_gated_trunk'


def rms_norm(x, gain=None):
    xf = x.astype(F32)
    y = xf * lax.rsqrt(jnp.mean(xf * xf, axis=-1, keepdims=True) + EPS)
    if gain is not None:
        y = y * gain.astype(F32)
    return y.astype(x.dtype)


def layer_norm(x, gain, bias):
    xf = x.astype(F32)
    xc = xf - jnp.mean(xf, axis=-1, keepdims=True)
    y = xc * lax.rsqrt(jnp.mean(xc * xc, axis=-1, keepdims=True) + EPS)
    return (y * gain.astype(F32) + bias.astype(F32)).astype(x.dtype)


def swiglu(h, w1, w3, w2):
    return (jax.nn.silu(h @ w1) * (h @ w3)) @ w2


def pool_mixer(a, w_pool, pool_scale):
    b, s, _ = a.shape
    af = a.astype(F32)
    cs = jnp.concatenate([jnp.zeros((b, 1, POOL_WIDTH), F32), jnp.cumsum(af, axis=1)], axis=1)
    t = jnp.arange(s)
    groups = []
    for g, win in enumerate(POOL_WINDOWS):
        ch = slice(g * POOL_GROUP, (g + 1) * POOL_GROUP)
        start = jnp.maximum(t + 1 - win, 0)
        count = (t + 1 - start).astype(F32)[None, :, None]
        mean = (cs[:, 1:, ch] - cs[:, start, ch]) / count
        groups.append(mean - af[:, :, ch])
    pooled = jnp.stack(groups, axis=2).astype(a.dtype)
    mixed = jnp.einsum('bsgc,gcd->bsgd', pooled, w_pool)
    return mixed.reshape(b, s, POOL_WIDTH) * pool_scale


def causal_dwconv(x, w, bias):
    y = lax.conv_general_dilated(x, w[:, None, :], window_strides=(1,), padding=[(SSD_CONV - 1, 0)],
                                 dimension_numbers=('NWC', 'WIO', 'NWC'), feature_group_count=x.shape[-1])
    return y + bias


def segsum(a):
    cs = jnp.cumsum(a, axis=-1)
    t = a.shape[-1]
    diff = cs[..., :, None] - cs[..., None, :]
    return jnp.where(jnp.tril(jnp.ones((t, t), bool)), diff, -jnp.inf)


def ssd_mixer(z, xbc, dt_raw, conv_w, conv_b, dt_bias, a_log, d_skip, norm_gain):
    b, s, _ = z.shape
    nc, ln = s // SSD_CHUNK, SSD_CHUNK
    hpg = SSD_HEADS // SSD_GROUPS
    xbc = jax.nn.silu(causal_dwconv(xbc, conv_w, conv_b))
    xs, bm, cm = jnp.split(xbc, [SSD_WIDTH, SSD_WIDTH + SSD_GROUPS * SSD_STATE], axis=-1)
    xs = xs.astype(F32).reshape(b, nc, ln, SSD_HEADS, SSD_HEADDIM)
    bm = bm.astype(F32).reshape(b, nc, ln, SSD_GROUPS, SSD_STATE)
    cm = cm.astype(F32).reshape(b, nc, ln, SSD_GROUPS, SSD_STATE)
    dt = jax.nn.softplus(dt_raw.astype(F32) + dt_bias.astype(F32)).reshape(b, nc, ln, SSD_HEADS)
    a = -jnp.exp(a_log.astype(F32))
    da = (dt * a).transpose(0, 3, 1, 2)
    xdt = xs * dt[..., None]
    a_cs = jnp.cumsum(da, axis=-1)
    cb = jnp.repeat(jnp.einsum('bclgn,bcsgn->bgcls', cm, bm), hpg, axis=1)
    y_diag = jnp.einsum('bhcls,bcshp->bclhp', cb * jnp.exp(segsum(da)), xdt)
    bh = jnp.repeat(bm, hpg, axis=3)
    ch = jnp.repeat(cm, hpg, axis=3)
    states = jnp.einsum('bclhn,bhcl,bclhp->bchpn', bh, jnp.exp(a_cs[..., -1:] - a_cs), xdt)
    def step(hstate, inp):
        decay, st = inp
        return hstate * decay[..., None, None] + st, hstate
    h0 = jnp.zeros((b, SSD_HEADS, SSD_HEADDIM, SSD_STATE), F32)
    _, prev = lax.scan(step, h0, (jnp.exp(a_cs[..., -1]).transpose(2, 0, 1), states.transpose(1, 0, 2, 3, 4)))
    y_off = jnp.einsum('bclhn,cbhpn,bhcl->bclhp', ch, prev, jnp.exp(a_cs))
    y = (y_diag + y_off + xs * d_skip.astype(F32)[:, None]).reshape(b, s, SSD_WIDTH)
    y = rms_norm(y * jax.nn.silu(z.astype(F32)), norm_gain)
    return y.astype(z.dtype)


def apply_rope(x, positions):
    half = MLA_ROPE // 2
    inv_freq = ROPE_THETA ** (-jnp.arange(half, dtype=F32) * 2.0 / MLA_ROPE)
    ang = positions.astype(F32)[:, :, None, None] * inv_freq
    cos, sin = jnp.cos(ang), jnp.sin(ang)
    xf = x.astype(F32)
    x1, x2 = xf[..., :half], xf[..., half:]
    return jnp.concatenate([x1 * cos - x2 * sin, x1 * sin + x2 * cos], axis=-1).astype(x.dtype)


def mla_mixer(cq, ckv, k_rope, positions, q_norm, w_uq, kv_norm, w_ukv, q_gain, k_gain):
    b, s, _ = cq.shape
    q = (rms_norm(cq, q_norm) @ w_uq).reshape(b, s, MLA_HEADS, MLA_QK)
    kv = (rms_norm(ckv, kv_norm) @ w_ukv).reshape(b, s, MLA_HEADS, MLA_NOPE + MLA_V)
    k_nope, v = kv[..., :MLA_NOPE], kv[..., MLA_NOPE:]
    k = jnp.concatenate([k_nope, jnp.broadcast_to(k_rope[:, :, None, :], (b, s, MLA_HEADS, MLA_ROPE))], axis=-1)
    q = rms_norm(q, q_gain)
    k = rms_norm(k, k_gain)
    q = jnp.concatenate([q[..., :MLA_NOPE], apply_rope(q[..., MLA_NOPE:], positions)], axis=-1)
    k = jnp.concatenate([k[..., :MLA_NOPE], apply_rope(k[..., MLA_NOPE:], positions)], axis=-1)
    q, k, v = q.transpose(0, 2, 1, 3), k.transpose(0, 2, 1, 3), v.transpose(0, 2, 1, 3)
    scale = MLA_QK ** -0.5
    kpos = jnp.arange(s)
    def query_block(i):
        qb = lax.dynamic_slice_in_dim(q, i * ATT_BLOCK, ATT_BLOCK, axis=2)
        sc = jnp.einsum('bhqd,bhkd->bhqk', qb, k).astype(F32) * scale
        qpos = i * ATT_BLOCK + jnp.arange(ATT_BLOCK)
        sc = jnp.where(kpos[None, :] <= qpos[:, None], sc, -jnp.inf)
        p = jax.nn.softmax(sc, axis=-1)
        return jnp.einsum('bhqk,bhkd->bhqd', p.astype(v.dtype), v)
    o = lax.map(query_block, jnp.arange(s // ATT_BLOCK))
    return o.transpose(1, 0, 3, 2, 4).reshape(b, s, MLA_HEADS * MLA_V)


def sgu_mixer(uv, ln_gain, ln_bias, w_s, b_s):
    b, s, _ = uv.shape
    uv = jax.nn.gelu(uv, approximate=False)
    u, v = jnp.split(uv, 2, axis=-1)
    v = layer_norm(v, ln_gain, ln_bias)
    v = v.reshape(b, s // SGU_CHUNK, SGU_CHUNK, SGU_GROUPS, SGU_GROUP_CH)
    w = w_s * jnp.tril(jnp.ones((SGU_CHUNK, SGU_CHUNK), w_s.dtype))
    mixed = jnp.einsum('gts,bnsgc->bntgc', w, v) + b_s.T[None, None, :, :, None]
    return u * mixed.reshape(b, s, SGU_WIDTH)


def moe_ffn(h, w_router, w1, w3, w2):
    b, s, d = h.shape
    n = b * s
    hf = h.reshape(n, d)
    logits = (hf @ w_router).astype(F32)
    top_logit, top_idx = lax.top_k(logits, TOP_K)
    top_w = jax.nn.softmax(top_logit, axis=-1)
    e_flat = top_idx.reshape(-1)
    tok_flat = jnp.arange(n * TOP_K, dtype=jnp.int32) // TOP_K
    order = jnp.argsort(e_flat)
    e_sorted = e_flat[order]
    counts = jnp.bincount(e_flat, length=N_EXPERTS)
    padded = (counts + MOE_BLOCK - 1) // MOE_BLOCK * MOE_BLOCK
    start = jnp.cumsum(counts) - counts
    pend = jnp.cumsum(padded)
    pstart = pend - padded
    slot = pstart[e_sorted] + jnp.arange(n * TOP_K) - start[e_sorted]
    n_slots = -(-(n * TOP_K) // MOE_BLOCK) * MOE_BLOCK + N_EXPERTS * MOE_BLOCK
    n_blocks = n_slots // MOE_BLOCK
    slot_tok = jnp.full((n_slots,), n, jnp.int32).at[slot].set(tok_flat[order])
    slot_w = jnp.zeros((n_slots,), F32).at[slot].set(top_w.reshape(-1)[order])
    block_e = jnp.minimum(jnp.searchsorted(pend, jnp.arange(n_blocks) * MOE_BLOCK, side='right'), N_EXPERTS - 1)
    h_pad = jnp.concatenate([hf, jnp.zeros((1, d), hf.dtype)], axis=0)
    xb = h_pad[slot_tok].reshape(n_blocks, MOE_BLOCK, d)
    yb = lax.map(lambda args: swiglu(args[0], w1[args[1]], w3[args[1]], w2[args[1]]), (xb, block_e))
    out = jnp.zeros((n + 1, d), F32).at[slot_tok].add(yb.reshape(n_slots, d).astype(F32) * slot_w[:, None])
    return out[:n].reshape(b, s, d).astype(h.dtype)


def setup_inputs(seed: int = 0) -> dict:
    key = jax.random.key(seed)
    keys = iter(jax.random.split(key, 48))
    def normal(shape, scale):
        return jax.random.normal(next(keys), shape, F32) * scale
    def near_one(shape):
        return 1.0 + normal(shape, 0.02)
    L = DEPTH
    x = normal((BATCH, SEQ, D_MODEL), 1.0)
    c = normal((BATCH, D_MODEL), 1.0)
    offset = jax.random.randint(next(keys), (BATCH, 1), 0, MAX_POS_OFFSET, jnp.int32)
    positions = offset + jnp.arange(SEQ, dtype=jnp.int32)[None, :]
    w_ada = normal((L, D_MODEL, 6 * D_MODEL), 0.5 * D_MODEL ** -0.5)
    b_ada = normal((L, 6 * D_MODEL), 0.02)
    w_in = normal((L, D_MODEL, D_IN), D_MODEL ** -0.5)
    w_pool = normal((L, len(POOL_WINDOWS), POOL_GROUP, POOL_GROUP), POOL_GROUP ** -0.5)
    pool_scale = near_one((L, POOL_WIDTH))
    ssd_conv_w = normal((L, SSD_CONV, SSD_CONV_DIM), SSD_CONV ** -0.5)
    ssd_conv_b = normal((L, SSD_CONV_DIM), 0.02)
    dt0 = jnp.exp(jax.random.uniform(next(keys), (L, SSD_HEADS), F32, math.log(DT_MIN), math.log(DT_MAX)))
    ssd_dt_bias = dt0 + jnp.log(-jnp.expm1(-dt0))
    ssd_a_log = jnp.log(jax.random.uniform(next(keys), (L, SSD_HEADS), F32, 1.0, 16.0))
    ssd_d = near_one((L, SSD_HEADS))
    ssd_norm = near_one((L, SSD_WIDTH))
    mla_q_norm = near_one((L, MLA_Q_LORA))
    mla_w_uq = normal((L, MLA_Q_LORA, MLA_HEADS * MLA_QK), MLA_Q_LORA ** -0.5)
    mla_kv_norm = near_one((L, MLA_KV_LORA))
    mla_w_ukv = normal((L, MLA_KV_LORA, MLA_HEADS * (MLA_NOPE + MLA_V)), MLA_KV_LORA ** -0.5)
    mla_q_gain = near_one((L, MLA_QK))
    mla_k_gain = near_one((L, MLA_QK))
    sgu_ln_gain = near_one((L, SGU_WIDTH))
    sgu_ln_bias = normal((L, SGU_WIDTH), 0.02)
    sgu_w_s = normal((L, SGU_GROUPS, SGU_CHUNK, SGU_CHUNK), SGU_CHUNK ** -0.5)
    sgu_b_s = near_one((L, SGU_GROUPS, SGU_CHUNK))
    w_branch = normal((L, N_BRANCH, MIX_WIDTH, D_MODEL), MIX_WIDTH ** -0.5)
    w_out = normal((L, D_MODEL, D_MODEL), D_MODEL ** -0.5)
    ffn_w1 = normal((N_DENSE, D_MODEL, D_FF), D_MODEL ** -0.5)
    ffn_w3 = normal((N_DENSE, D_MODEL, D_FF), D_MODEL ** -0.5)
    ffn_w2 = normal((N_DENSE, D_FF, D_MODEL), D_FF ** -0.5)
    moe_router = normal((N_MOE, D_MODEL, N_EXPERTS), D_MODEL ** -0.5)
    moe_w1 = normal((N_MOE, N_EXPERTS, D_MODEL, D_FF_EXPERT), D_MODEL ** -0.5)
    moe_w3 = normal((N_MOE, N_EXPERTS, D_MODEL, D_FF_EXPERT), D_MODEL ** -0.5)
    moe_w2 = normal((N_MOE, N_EXPERTS, D_FF_EXPERT, D_MODEL), D_FF_EXPERT ** -0.5)
    return {'x': x, 'c': c, 'positions': positions, 'w_ada': w_ada, 'b_ada': b_ada, 'w_in': w_in,
            'w_pool': w_pool, 'pool_scale': pool_scale, 'ssd_conv_w': ssd_conv_w, 'ssd_conv_b': ssd_conv_b,
            'ssd_dt_bias': ssd_dt_bias, 'ssd_a_log': ssd_a_log, 'ssd_d': ssd_d, 'ssd_norm': ssd_norm,
            'mla_q_norm': mla_q_norm, 'mla_w_uq': mla_w_uq, 'mla_kv_norm': mla_kv_norm, 'mla_w_ukv': mla_w_ukv,
            'mla_q_gain': mla_q_gain, 'mla_k_gain': mla_k_gain, 'sgu_ln_gain': sgu_ln_gain,
            'sgu_ln_bias': sgu_ln_bias, 'sgu_w_s': sgu_w_s, 'sgu_b_s': sgu_b_s, 'w_branch': w_branch,
            'w_out': w_out, 'ffn_w1': ffn_w1, 'ffn_w3': ffn_w3, 'ffn_w2': ffn_w2, 'moe_router': moe_router,
            'moe_w1': moe_w1, 'moe_w3': moe_w3, 'moe_w2': moe_w2}


def reference(x, c, positions, w_ada, b_ada, w_in, w_pool, pool_scale, ssd_conv_w, ssd_conv_b, ssd_dt_bias,
              ssd_a_log, ssd_d, ssd_norm, mla_q_norm, mla_w_uq, mla_kv_norm, mla_w_ukv, mla_q_gain, mla_k_gain,
              sgu_ln_gain, sgu_ln_bias, sgu_w_s, sgu_b_s, w_branch, w_out, ffn_w1, ffn_w3, ffn_w2, moe_router,
              moe_w1, moe_w3, moe_w2):
    b, s, _ = x.shape
    offsets = np.cumsum(IN_SPLITS)[:-1].tolist()
    c_act = jax.nn.silu(c)
    for layer in range(DEPTH):
        mod = (c_act @ w_ada[layer] + b_ada[layer])[:, None, :]
        shift1, scale1, gate1, shift2, scale2, gate2 = jnp.split(mod, 6, axis=-1)
        h = rms_norm(x) * (1 + scale1) + shift1
        proj = h @ w_in[layer]
        a_in, z, xbc, dt_raw, cq, ckv, k_rope, uv, gate_logits = jnp.split(proj, offsets, axis=-1)
        y_pool = pool_mixer(a_in, w_pool[layer], pool_scale[layer])
        y_ssd = ssd_mixer(z, xbc, dt_raw, ssd_conv_w[layer], ssd_conv_b[layer], ssd_dt_bias[layer],
                          ssd_a_log[layer], ssd_d[layer], ssd_norm[layer])
        y_att = mla_mixer(cq, ckv, k_rope, positions, mla_q_norm[layer], mla_w_uq[layer], mla_kv_norm[layer],
                          mla_w_ukv[layer], mla_q_gain[layer], mla_k_gain[layer])
        y_sgu = sgu_mixer(uv, sgu_ln_gain[layer], sgu_ln_bias[layer], sgu_w_s[layer], sgu_b_s[layer])
        gates = jax.nn.sigmoid(gate_logits).reshape(b, s, N_BRANCH, D_MODEL)
        merged = sum(gates[:, :, i] * (y @ w_branch[layer, i]) for i, y in enumerate((y_pool, y_ssd, y_att, y_sgu)))
        x = x + gate1 * (merged @ w_out[layer])
        h2 = rms_norm(x) * (1 + scale2) + shift2
        idx = layer // 2
        if layer % 2 == 0:
            f = swiglu(h2, ffn_w1[idx], ffn_w3[idx], ffn_w2[idx])
        else:
            f = moe_ffn(h2, moe_router[idx], moe_w1[idx], moe_w3[idx], moe_w2[idx])
        x = x + gate2 * f
    return x
```

```python
import functools

import numpy as np
import jax
import jax.numpy as jnp
from jax import lax
from jax.experimental import pallas as pl
from jax.experimental.pallas import tpu as pltpu

F32 = jnp.float32
BF16 = jnp.bfloat16
I32 = jnp.int32
EPS = 1e-6

D_MODEL = 2048
DEPTH = 2
MIX_WIDTH = D_MODEL // 2
N_BRANCH = 4

POOL_WINDOWS = (2, 4, 8, 16)
POOL_GROUP = MIX_WIDTH // len(POOL_WINDOWS)

SSD_HEADDIM = 64
SSD_HEADS = MIX_WIDTH // SSD_HEADDIM
SSD_STATE = 128
SSD_GROUPS = 2
SSD_CONV = 4
SSD_CHUNK = 128
SSD_CONV_DIM = MIX_WIDTH + 2 * SSD_GROUPS * SSD_STATE

MLA_HEADS = 8
MLA_NOPE = 128
MLA_ROPE = 64
MLA_QK = MLA_NOPE + MLA_ROPE
MLA_V = MIX_WIDTH // MLA_HEADS
MLA_Q_LORA = 768
MLA_KV_LORA = 512
ROPE_THETA = 10000.0
MLA_PAD = 256

SGU_CHUNK = 128
SGU_GROUPS = 8

D_FF = 5632
N_EXPERTS = 8
D_FF_EXPERT = 7168
MOE_BLOCK = 512
E_PAD = 16

IN_SPLITS = (MIX_WIDTH, MIX_WIDTH, SSD_CONV_DIM, SSD_HEADS, MLA_Q_LORA, MLA_KV_LORA, MLA_ROPE,
             2 * MIX_WIDTH, N_BRANCH * D_MODEL)
IN_OFFS = tuple(int(v) for v in np.cumsum((0,) + IN_SPLITS))

P_UV, P_AIN, P_XBC, P_CQ, P_CKV, P_Z, P_TOTAL = 0, 2048, 3072, 4608, 5632, 6144, 7168
LANE = 128
VMEM_LIMIT = 56 * 1024 * 1024

NT_DIMS = (((1,), (1,)), ((), ()))


def _cp(*sem):
    return pltpu.CompilerParams(dimension_semantics=sem, vmem_limit_bytes=VMEM_LIMIT)


def _sigmoid(v):
    return 1.0 / (1.0 + jnp.exp(-v))


def _dot(a, b):
    return jnp.dot(a, b, preferred_element_type=F32)


def _split_bf16(v):
    hi = v.astype(BF16)
    lo = (v - hi.astype(F32)).astype(BF16)
    return hi, lo


def _normmod(xv, scale, shift):
    ms = jnp.mean(xv * xv, axis=-1, keepdims=True)
    return (xv * lax.rsqrt(ms + EPS)) * (1.0 + scale) + shift


def _ada_kernel(c_ref, w_ref, b_ref, o_ref):
    c = c_ref[...]
    ca = c * _sigmoid(c)
    o_ref[0] = jnp.sum(w_ref[0] * ca, axis=0, keepdims=True) + b_ref[0]


def ada_mod(c, w_ada, b_ada):
    nl, d, n = w_ada.shape
    tn = 1024
    out = pl.pallas_call(
        _ada_kernel, grid=(nl, n // tn),
        in_specs=[pl.BlockSpec((d, 1), lambda l, j: (0, 0)),
                  pl.BlockSpec((1, d, tn), lambda l, j: (l, 0, j)),
                  pl.BlockSpec((1, 1, tn), lambda l, j: (l, 0, j))],
        out_specs=pl.BlockSpec((1, 1, tn), lambda l, j: (l, 0, j)),
        out_shape=jax.ShapeDtypeStruct((nl, 1, n), F32),
        compiler_params=_cp("arbitrary", "arbitrary"), name="ada_mod",
    )(c.reshape(d, 1), w_ada, b_ada.reshape(nl, 1, n))
    return out.reshape(nl, n)


def _normmod_kernel(x_ref, sc_ref, sh_ref, h_ref):
    h_ref[...] = _normmod(x_ref[...], sc_ref[...], sh_ref[...]).astype(BF16)


def norm_mod(x, scale, shift, tm=512):
    s, d = x.shape
    row = pl.BlockSpec((1, d), lambda i: (0, 0))
    return pl.pallas_call(
        _normmod_kernel, grid=(s // tm,),
        in_specs=[pl.BlockSpec((tm, d), lambda i: (i, 0)), row, row],
        out_specs=pl.BlockSpec((tm, d), lambda i: (i, 0)),
        out_shape=jax.ShapeDtypeStruct((s, d), BF16),
        compiler_params=_cp("arbitrary"), name="norm_mod",
    )(x, scale, shift)


def _mm_kernel(a_ref, w_ref, o_ref):
    o_ref[...] = _dot(a_ref[...], w_ref[...]).astype(o_ref.dtype)


def matmul(a, w, out_dtype, tm, tn, name):
    m, k = a.shape
    n = w.shape[1]
    return pl.pallas_call(
        _mm_kernel, grid=(n // tn, m // tm),
        in_specs=[pl.BlockSpec((tm, k), lambda j, i: (i, 0)),
                  pl.BlockSpec((k, tn), lambda j, i: (0, j))],
        out_specs=pl.BlockSpec((tm, tn), lambda j, i: (i, j)),
        out_shape=jax.ShapeDtypeStruct((m, n), out_dtype),
        compiler_params=_cp("arbitrary", "arbitrary"), name=name,
    )(a, w)


def _pool_kernel(halo_ref, a_ref, w_ref, s_ref, o_ref, *, tm):
    i = pl.program_id(0)
    ext = jnp.concatenate([halo_ref[...], a_ref[...]], axis=0)
    kk = tm + LANE
    r = lax.broadcasted_iota(I32, (tm, kk), 0)
    k = lax.broadcasted_iota(I32, (tm, kk), 1)
    d = r - k + LANE
    t = r + i * tm
    src_ok = (t - d) >= 0
    for g, win in enumerate(POOL_WINDOWS):
        cnt = jnp.minimum(t + 1, win).astype(F32)
        inwin = (d >= 0) & (d < win) & src_ok
        coef = jnp.where(inwin, 1.0 / cnt, 0.0) - jnp.where(d == 0, 1.0, 0.0)
        sl = slice(g * POOL_GROUP, (g + 1) * POOL_GROUP)
        pooled = _dot(coef.astype(BF16), ext[:, sl])
        mixed = _dot(pooled.astype(BF16), w_ref[g])
        o_ref[:, sl] = (mixed * s_ref[:, sl]).astype(BF16)


def pool_mixer(proj, w_pool, pool_scale, tm=512):
    s = proj.shape[0]
    hb = tm // LANE
    return pl.pallas_call(
        functools.partial(_pool_kernel, tm=tm), grid=(s // tm,),
        in_specs=[pl.BlockSpec((LANE, MIX_WIDTH), lambda i: (jnp.maximum(i * hb - 1, 0), P_AIN // MIX_WIDTH)),
                  pl.BlockSpec((tm, MIX_WIDTH), lambda i: (i, P_AIN // MIX_WIDTH)),
                  pl.BlockSpec((4, POOL_GROUP, POOL_GROUP), lambda i: (0, 0, 0)),
                  pl.BlockSpec((1, MIX_WIDTH), lambda i: (0, 0))],
        out_specs=pl.BlockSpec((tm, MIX_WIDTH), lambda i: (i, 0)),
        out_shape=jax.ShapeDtypeStruct((s, MIX_WIDTH), BF16),
        compiler_params=_cp("arbitrary"), name="pool_mixer",
    )(proj, proj, w_pool, pool_scale)


def _sgu_kernel(uv_ref, g_ref, b_ref, ws_ref, bs_ref, o_ref, *, tm):
    uv = uv_ref[...].astype(F32)
    uv = 0.5 * uv * (1.0 + lax.erf(uv * (2.0 ** -0.5)))
    u = uv[:, :MIX_WIDTH]
    v = uv[:, MIX_WIDTH:]
    mu = jnp.mean(v, axis=-1, keepdims=True)
    vc = v - mu
    var = jnp.mean(vc * vc, axis=-1, keepdims=True)
    vn = (vc * lax.rsqrt(var + EPS) * g_ref[...] + b_ref[...]).astype(BF16)
    row = lax.broadcasted_iota(I32, (SGU_CHUNK, SGU_CHUNK), 0)
    col = lax.broadcasted_iota(I32, (SGU_CHUNK, SGU_CHUNK), 1)
    for g in range(SGU_GROUPS):
        w = jnp.where(row >= col, ws_ref[g], 0.0).astype(BF16)
        bcol = bs_ref[:, g:g + 1]
        cs = slice(g * LANE, (g + 1) * LANE)
        for c in range(tm // SGU_CHUNK):
            rs = slice(c * SGU_CHUNK, (c + 1) * SGU_CHUNK)
            mixed = _dot(w, vn[rs, cs]) + bcol
            o_ref[rs, cs] = (u[rs, cs] * mixed).astype(BF16)


def sgu_mixer(proj, ln_gain, ln_bias, w_s, b_s_t, tm=512):
    s = proj.shape[0]
    row = pl.BlockSpec((1, MIX_WIDTH), lambda i: (0, 0))
    return pl.pallas_call(
        functools.partial(_sgu_kernel, tm=tm), grid=(s // tm,),
        in_specs=[pl.BlockSpec((tm, 2 * MIX_WIDTH), lambda i: (i, P_UV // (2 * MIX_WIDTH))), row, row,
                  pl.BlockSpec((SGU_GROUPS, SGU_CHUNK, SGU_CHUNK), lambda i: (0, 0, 0)),
                  pl.BlockSpec((SGU_CHUNK, SGU_GROUPS), lambda i: (0, 0))],
        out_specs=pl.BlockSpec((tm, MIX_WIDTH), lambda i: (i, 0)),
        out_shape=jax.ShapeDtypeStruct((s, MIX_WIDTH), BF16),
        compiler_params=_cp("arbitrary"), name="sgu_mixer",
    )(proj, ln_gain, ln_bias, w_s, b_s_t)


def _rope_kernel(pos_ref, inv_ref, cos_ref, sin_ref):
    ang = pos_ref[...] * inv_ref[...]
    cos_ref[...] = jnp.cos(ang)
    sin_ref[...] = jnp.sin(ang)


def rope_table(pos_col, inv_row, tm=512):
    s = pos_col.shape[0]
    out = pl.BlockSpec((tm, LANE), lambda i: (i, 0))
    return pl.pallas_call(
        _rope_kernel, grid=(s // tm,),
        in_specs=[pl.BlockSpec((tm, 1), lambda i: (i, 0)), pl.BlockSpec((1, LANE), lambda i: (0, 0))],
        out_specs=[out, out],
        out_shape=[jax.ShapeDtypeStruct((s, LANE), F32)] * 2,
        compiler_params=_cp("arbitrary"), name="rope_table",
    )(pos_col, inv_row)


def _mla_prep_kernel(cq_ref, ckv_ref, kr_ref, cos_ref, sin_ref, qn_ref, kvn_ref, wq_ref, wk_ref, wv_ref,
                     qg_ref, kg_ref, q_out, k_out, vt_out, *, tm, tk):
    cq = cq_ref[...].astype(F32)
    cqn = (cq * lax.rsqrt(jnp.mean(cq * cq, axis=-1, keepdims=True) + EPS) * qn_ref[...]).astype(BF16)
    ckv = ckv_ref[...].astype(F32)
    ckvn = (ckv * lax.rsqrt(jnp.mean(ckv * ckv, axis=-1, keepdims=True) + EPS) * kvn_ref[...]).astype(BF16)
    q = _dot(cqn, wq_ref[...])
    kn = _dot(ckvn, wk_ref[...])
    v = _dot(ckvn, wv_ref[...])
    kr = kr_ref[...]
    cos = cos_ref[...]
    lane = lax.broadcasted_iota(I32, (tm, LANE), 1)
    half = MLA_ROPE // 2
    sin_signed = jnp.where(lane < half, -sin_ref[...], sin_ref[...])

    def rope(xr):
        swapped = jnp.where(lane < half, pltpu.roll(xr, LANE - half, 1), pltpu.roll(xr, half, 1))
        return xr * cos + swapped * sin_signed

    kr_ss = jnp.sum(kr * kr, axis=-1, keepdims=True)
    qg = qg_ref[...]
    kg = kg_ref[...]
    scale = MLA_QK ** -0.5
    inv_qk = 1.0 / MLA_QK
    for h in range(MLA_HEADS):
        qh = q[:, h * MLA_PAD:(h + 1) * MLA_PAD]
        rq = lax.rsqrt(jnp.sum(qh * qh, axis=-1, keepdims=True) * inv_qk + EPS) * scale
        qh = qh * rq * qg
        q_out[h, :, :LANE] = qh[:, :LANE].astype(BF16)
        q_out[h, :, LANE:] = rope(qh[:, LANE:]).astype(BF16)
        knh = kn[:, h * LANE:(h + 1) * LANE]
        rk = lax.rsqrt((jnp.sum(knh * knh, axis=-1, keepdims=True) + kr_ss) * inv_qk + EPS)
        k_out[h, :, :LANE] = (knh * rk * kg[:, :LANE]).astype(BF16)
        k_out[h, :, LANE:] = rope(kr * rk * kg[:, LANE:]).astype(BF16)
        vh = v[:, h * LANE:(h + 1) * LANE]
        for c in range(tm // tk):
            vt_out[h, c] = vh[c * tk:(c + 1) * tk].T.astype(BF16)


def mla_prep(proj, misc, cos, sin, q_norm, kv_norm, wq, wk, wv, q_gain, k_gain, tm=512, tk=512):
    s = proj.shape[0]
    full = lambda shape: pl.BlockSpec(shape, lambda i: tuple(0 for _ in shape))
    return pl.pallas_call(
        functools.partial(_mla_prep_kernel, tm=tm, tk=tk), grid=(s // tm,),
        in_specs=[pl.BlockSpec((tm, MLA_Q_LORA), lambda i: (i, P_CQ // MLA_Q_LORA)),
                  pl.BlockSpec((tm, MLA_KV_LORA), lambda i: (i, P_CKV // MLA_KV_LORA)),
                  pl.BlockSpec((tm, LANE), lambda i: (i, 1)),
                  pl.BlockSpec((tm, LANE), lambda i: (i, 0)),
                  pl.BlockSpec((tm, LANE), lambda i: (i, 0)),
                  full((1, MLA_Q_LORA)), full((1, MLA_KV_LORA)),
                  full((MLA_Q_LORA, MLA_HEADS * MLA_PAD)), full((MLA_KV_LORA, MIX_WIDTH)),
                  full((MLA_KV_LORA, MIX_WIDTH)), full((1, MLA_PAD)), full((1, MLA_PAD))],
        out_specs=[pl.BlockSpec((MLA_HEADS, tm, MLA_PAD), lambda i: (0, i, 0)),
                   pl.BlockSpec((MLA_HEADS, tm, MLA_PAD), lambda i: (0, i, 0)),
                   pl.BlockSpec((MLA_HEADS, tm // tk, MLA_V, tk), lambda i: (0, i, 0, 0))],
        out_shape=[jax.ShapeDtypeStruct((MLA_HEADS, s, MLA_PAD), BF16),
                   jax.ShapeDtypeStruct((MLA_HEADS, s, MLA_PAD), BF16),
                   jax.ShapeDtypeStruct((MLA_HEADS, s // tk, MLA_V, tk), BF16)],
        compiler_params=_cp("arbitrary"), name="mla_prep",
    )(proj, proj, misc, cos, sin, q_norm, kv_norm, wq, wk, wv, q_gain, k_gain)


def _flash_kernel(q_ref, k_ref, vt_ref, o_ref, m_ref, l_ref, acc_ref, *, t):
    i = pl.program_id(1)
    q = q_ref[0]
    m_ref[...] = jnp.full_like(m_ref, -jnp.inf)
    l_ref[...] = jnp.zeros_like(l_ref)
    acc_ref[...] = jnp.zeros_like(acc_ref)

    def step(j, masked):
        kc = k_ref[0, pl.ds(pl.multiple_of(j * t, t), t), :]
        st = lax.dot_general(kc, q, NT_DIMS, preferred_element_type=F32)
        if masked:
            kk = lax.broadcasted_iota(I32, (t, t), 0)
            qq = lax.broadcasted_iota(I32, (t, t), 1)
            st = jnp.where(kk <= qq, st, -jnp.inf)
        m_prev = m_ref[...]
        m_new = jnp.maximum(m_prev, jnp.max(st, axis=0, keepdims=True))
        p = jnp.exp(st - m_new)
        alpha = jnp.exp(m_prev - m_new)
        l_ref[...] = alpha * l_ref[...] + jnp.sum(p, axis=0, keepdims=True)
        acc_ref[...] = acc_ref[...] * alpha + _dot(vt_ref[0, j], p.astype(BF16))
        m_ref[...] = m_new

    def body(j, carry):
        step(j, False)
        return carry

    lax.fori_loop(0, i, body, 0)
    step(i, True)
    o_ref[...] = (acc_ref[...] * (1.0 / l_ref[...])).T.astype(BF16)


def flash_attention(q, k, vt, t=512):
    nh, s, _ = q.shape
    return pl.pallas_call(
        functools.partial(_flash_kernel, t=t), grid=(nh, s // t),
        in_specs=[pl.BlockSpec((1, t, MLA_PAD), lambda h, i: (h, i, 0)),
                  pl.BlockSpec((1, s, MLA_PAD), lambda h, i: (h, 0, 0)),
                  pl.BlockSpec((1, s // t, MLA_V, t), lambda h, i: (h, 0, 0, 0))],
        out_specs=pl.BlockSpec((t, MLA_V), lambda h, i: (i, h)),
        out_shape=jax.ShapeDtypeStruct((s, nh * MLA_V), BF16),
        scratch_shapes=[pltpu.VMEM((1, t), F32), pltpu.VMEM((1, t), F32), pltpu.VMEM((MLA_V, t), F32)],
        compiler_params=_cp("arbitrary", "arbitrary"), name="mla_flash",
    )(q, k, vt)


def _ssd_kernel(z_ref, xbc_ref, halo_ref, dt_ref, cw_ref, cb_ref, dtb_ref, alog_ref, dx_ref, ng_ref, e_ref,
                o_ref, ht_ref, xs_ref, bc_ref, dts_ref, y_ref, *, tm):
    i = pl.program_id(0)
    ck = SSD_CHUNK
    hd2 = 2 * SSD_HEADDIM
    gw = MIX_WIDTH // SSD_GROUPS

    @pl.when(i == 0)
    def _():
        ht_ref[...] = jnp.zeros_like(ht_ref)

    halo = jnp.where(i > 0, halo_ref[...].astype(F32), 0.0)
    nh = halo.shape[0]
    xe = jnp.concatenate([halo, xbc_ref[...].astype(F32)], axis=0)
    cw = cw_ref[...]
    conv = cb_ref[...]
    for tap in range(SSD_CONV):
        off = nh - (SSD_CONV - 1) + tap
        conv = conv + cw[tap:tap + 1] * xe[off:off + tm]
    act = conv * _sigmoid(conv)
    xs_ref[...] = act[:, :MIX_WIDTH]
    bc_ref[...] = act[:, MIX_WIDTH:]
    dtv = dt_ref[...] + dtb_ref[...]
    dts_ref[...] = jnp.maximum(dtv, 0.0) + jnp.log1p(jnp.exp(-jnp.abs(dtv)))
    a_row = -jnp.exp(alog_ref[...])

    li = lax.broadcasted_iota(I32, (ck, ck), 0)
    si = lax.broadcasted_iota(I32, (ck, ck), 1)
    causal = li >= si
    tril = jnp.where(causal, 1.0, 0.0).astype(BF16)
    low_half = si < SSD_HEADDIM
    expand_m = e_ref[...]

    def expand(val):
        hi, lo = _split_bf16(val)
        return _dot(hi, expand_m) + _dot(lo, expand_m)

    def chunk(c, carry):
        r0 = pl.multiple_of(c * ck, ck)
        rows = pl.ds(r0, ck)
        xs = xs_ref[rows, :]
        dt = dts_ref[rows, :]
        da_hi, da_lo = _split_bf16(dt * a_row)
        acs = _dot(tril, da_hi) + _dot(tril, da_lo)
        acs_t = acs.T
        eacs_x = expand(jnp.exp(acs))
        dec_x = expand(jnp.exp(acs[ck - 1:ck, :] - acs))
        xdt = xs * expand(dt)
        xdt_b = xdt.astype(BF16)
        xdd_b = (xdt * dec_x).astype(BF16)
        ht = ht_ref[...]
        ht_b = ht.astype(BF16)
        for g in range(SSD_GROUPS):
            gs = slice(g * gw, (g + 1) * gw)
            bmg = bc_ref[rows, g * SSD_STATE:(g + 1) * SSD_STATE]
            cmg_b = bc_ref[rows, (SSD_GROUPS + g) * SSD_STATE:(SSD_GROUPS + g + 1) * SSD_STATE].astype(BF16)
            cbm = lax.dot_general(cmg_b, bmg.astype(BF16), NT_DIMS, preferred_element_type=F32)
            yoff = _dot(cmg_b, ht_b[:, gs]) * eacs_x[:, gs]
            for qd in range(gw // hd2):
                col = g * gw + qd * hd2
                cs = slice(col, col + hd2)
                xp = xdt_b[:, cs]
                pair = []
                for hh in (2 * (col // hd2), 2 * (col // hd2) + 1):
                    seg = acs[:, hh:hh + 1] - acs_t[hh:hh + 1, :]
                    decay = jnp.exp(jnp.where(causal, seg, -jnp.inf))
                    pair.append(_dot((cbm * decay).astype(BF16), xp))
                ydiag = jnp.where(low_half, pair[0], pair[1])
                y_ref[rows, cs] = ydiag + yoff[:, qd * hd2:(qd + 1) * hd2] + xs[:, cs] * dx_ref[:, cs]
            st = _dot(bmg.T.astype(BF16), xdd_b[:, gs])
            ht_ref[:, gs] = ht[:, gs] * eacs_x[ck - 1:ck, gs] + st
        return carry

    lax.fori_loop(0, tm // ck, chunk, 0)
    z = z_ref[...].astype(F32)
    yz = y_ref[...] * (z * _sigmoid(z))
    o_ref[...] = (yz * lax.rsqrt(jnp.mean(yz * yz, axis=-1, keepdims=True) + EPS) * ng_ref[...]).astype(BF16)


def ssd_mixer(proj, misc, conv_w, conv_b, dt_bias, a_log, d_x, norm_gain, expand_m, tm=512):
    s = proj.shape[0]
    nh = 16
    full = lambda shape: pl.BlockSpec(shape, lambda i: tuple(0 for _ in shape))
    return pl.pallas_call(
        functools.partial(_ssd_kernel, tm=tm), grid=(s // tm,),
        in_specs=[pl.BlockSpec((tm, MIX_WIDTH), lambda i: (i, P_Z // MIX_WIDTH)),
                  pl.BlockSpec((tm, SSD_CONV_DIM), lambda i: (i, P_XBC // SSD_CONV_DIM)),
                  pl.BlockSpec((nh, SSD_CONV_DIM),
                               lambda i: (jnp.maximum(i * (tm // nh) - 1, 0), P_XBC // SSD_CONV_DIM)),
                  pl.BlockSpec((tm, LANE), lambda i: (i, 0)),
                  full((SSD_CONV, SSD_CONV_DIM)), full((1, SSD_CONV_DIM)), full((1, LANE)), full((1, LANE)),
                  full((1, MIX_WIDTH)), full((1, MIX_WIDTH)), full((LANE, MIX_WIDTH))],
        out_specs=pl.BlockSpec((tm, MIX_WIDTH), lambda i: (i, 0)),
        out_shape=jax.ShapeDtypeStruct((s, MIX_WIDTH), BF16),
        scratch_shapes=[pltpu.VMEM((SSD_STATE, MIX_WIDTH), F32), pltpu.VMEM((tm, MIX_WIDTH), F32),
                        pltpu.VMEM((tm, 2 * SSD_GROUPS * SSD_STATE), F32), pltpu.VMEM((tm, LANE), F32),
                        pltpu.VMEM((tm, MIX_WIDTH), F32)],
        compiler_params=_cp("arbitrary"), name="ssd_mixer",
    )(proj, proj, proj, misc, conv_w, conv_b, dt_bias, a_log, d_x, norm_gain, expand_m)


def _merge_kernel(h_ref, yp_ref, ys_ref, ya_ref, yg_ref, g0_ref, g1_ref, g2_ref, g3_ref, wb_ref, o_ref):
    h = h_ref[...]
    acc = None
    for b, (y_ref, wg_ref) in enumerate(((yp_ref, g0_ref), (ys_ref, g1_ref), (ya_ref, g2_ref), (yg_ref, g3_ref))):
        term = _sigmoid(_dot(h, wg_ref[...])) * _dot(y_ref[...], wb_ref[b])
        acc = term if acc is None else acc + term
    o_ref[...] = acc.astype(BF16)


def merge_branches(h, ys, w_gate, w_branch, tm=512, tn=512):
    s, d = h.shape
    nj = d // tn
    ysp = pl.BlockSpec((tm, MIX_WIDTH), lambda j, i: (i, 0))
    gate_specs = [pl.BlockSpec((d, tn), functools.partial(lambda j, i, b: (0, b * nj + j), b=b))
                  for b in range(N_BRANCH)]
    return pl.pallas_call(
        _merge_kernel, grid=(nj, s // tm),
        in_specs=[pl.BlockSpec((tm, d), lambda j, i: (i, 0)), ysp, ysp, ysp, ysp, *gate_specs,
                  pl.BlockSpec((N_BRANCH, MIX_WIDTH, tn), lambda j, i: (0, 0, j))],
        out_specs=pl.BlockSpec((tm, tn), lambda j, i: (i, j)),
        out_shape=jax.ShapeDtypeStruct((s, d), BF16),
        compiler_params=_cp("arbitrary", "arbitrary"), name="merge_branches",
    )(h, *ys, w_gate, w_gate, w_gate, w_gate, w_branch)


def _resid_kernel(m_ref, w_ref, x_ref, g_ref, sc_ref, sh_ref, xo_ref, ho_ref):
    xn = x_ref[...] + g_ref[...] * _dot(m_ref[...], w_ref[...])
    xo_ref[...] = xn
    ho_ref[...] = _normmod(xn, sc_ref[...], sh_ref[...]).astype(BF16)


def resid_out(merged, w_out, x, gate, scale, shift, tm=512):
    s, d = x.shape
    row = pl.BlockSpec((1, d), lambda i: (0, 0))
    tile = pl.BlockSpec((tm, d), lambda i: (i, 0))
    return pl.pallas_call(
        _resid_kernel, grid=(s // tm,),
        in_specs=[tile, pl.BlockSpec((d, d), lambda i: (0, 0)), tile, row, row, row],
        out_specs=[tile, tile],
        out_shape=[jax.ShapeDtypeStruct((s, d), F32), jax.ShapeDtypeStruct((s, d), BF16)],
        compiler_params=_cp("arbitrary"), name="resid_out",
    )(merged, w_out, x, gate, scale, shift)


def _ffn_kernel(h_ref, w1_ref, w3_ref, w2_ref, x_ref, g_ref, sc_ref, sh_ref, xo_ref, ho_ref, acc_ref):
    j = pl.program_id(1)

    @pl.when(j == 0)
    def _():
        acc_ref[...] = jnp.zeros_like(acc_ref)

    h = h_ref[...]
    a = _dot(h, w1_ref[...])
    act = (a * _sigmoid(a) * _dot(h, w3_ref[...])).astype(BF16)
    acc_ref[...] += _dot(act, w2_ref[...])

    @pl.when(j == pl.num_programs(1) - 1)
    def _():
        xn = x_ref[...] + g_ref[...] * acc_ref[...]
        xo_ref[...] = xn
        ho_ref[...] = _normmod(xn, sc_ref[...], sh_ref[...]).astype(BF16)


def dense_ffn(h2, w1, w3, w2, x, gate, scale, shift, tm=512, tj=512):
    s, d = x.shape
    ff = w1.shape[1]
    row = pl.BlockSpec((1, d), lambda i, j: (0, 0))
    tile = pl.BlockSpec((tm, d), lambda i, j: (i, 0))
    return pl.pallas_call(
        _ffn_kernel, grid=(s // tm, ff // tj),
        in_specs=[tile, pl.BlockSpec((d, tj), lambda i, j: (0, j)), pl.BlockSpec((d, tj), lambda i, j: (0, j)),
                  pl.BlockSpec((tj, d), lambda i, j: (j, 0)), tile, row, row, row],
        out_specs=[tile, tile],
        out_shape=[jax.ShapeDtypeStruct((s, d), F32), jax.ShapeDtypeStruct((s, d), BF16)],
        scratch_shapes=[pltpu.VMEM((tm, d), F32)],
        compiler_params=_cp("arbitrary", "arbitrary"), name="dense_ffn",
    )(h2, w1, w3, w2, x, gate, scale, shift)


def _router_kernel(x_ref, sc_ref, sh_ref, whi_ref, wlo_ref, info_ref, base_ref, cnt_ref, carry_ref, *, tm):
    i = pl.program_id(0)

    @pl.when(i == 0)
    def _():
        carry_ref[...] = jnp.zeros_like(carry_ref)

    h = _normmod(x_ref[...], sc_ref[...], sh_ref[...])
    h_hi, h_lo = _split_bf16(h)
    nt = functools.partial(lax.dot_general, dimension_numbers=NT_DIMS, preferred_element_type=F32)
    whi = whi_ref[...]
    lg = nt(whi, h_hi) + nt(whi, h_lo) + nt(wlo_ref[...], h_hi)
    row = lax.broadcasted_iota(I32, (E_PAD, tm), 0)
    lg = jnp.where(row < N_EXPERTS, lg, -jnp.inf)
    m1 = jnp.max(lg, axis=0, keepdims=True)
    i1 = jnp.min(jnp.where(lg == m1, row, E_PAD), axis=0, keepdims=True)
    lg2 = jnp.where(row == i1, -jnp.inf, lg)
    m2 = jnp.max(lg2, axis=0, keepdims=True)
    i2 = jnp.min(jnp.where(lg2 == m2, row, E_PAD), axis=0, keepdims=True)
    e21 = jnp.exp(m2 - m1)
    w1 = 1.0 / (1.0 + e21)
    w2 = e21 * w1
    pick1 = row == i1
    pick2 = row == i2
    sel = jnp.where(pick1, 1.0, jnp.where(pick2, 1.0, 0.0))
    before = lax.broadcasted_iota(I32, (tm, tm), 0) < lax.broadcasted_iota(I32, (tm, tm), 1)
    base = carry_ref[...]
    cum = _dot(sel.astype(BF16), jnp.where(before, 1.0, 0.0).astype(BF16)) + base[:, 0:1]
    r1 = jnp.sum(jnp.where(pick1, cum, 0.0), axis=0, keepdims=True)
    r2 = jnp.sum(jnp.where(pick2, cum, 0.0), axis=0, keepdims=True)
    r8 = lax.broadcasted_iota(I32, (8, tm), 0)
    info = jnp.where(r8 == 0, i1.astype(F32), jnp.where(r8 == 1, i2.astype(F32), jnp.where(
        r8 == 2, r1, jnp.where(r8 == 3, r2, jnp.where(r8 == 4, w1, jnp.where(r8 == 5, w2, 0.0))))))
    info_ref[...] = info
    base_ref[0] = base
    total = base + jnp.sum(sel, axis=1, keepdims=True)
    carry_ref[...] = total
    cnt_ref[...] = total


def moe_router(x, scale, shift, wt_hi, wt_lo, tm=512):
    s, d = x.shape
    row = pl.BlockSpec((1, d), lambda i: (0, 0))
    wsp = pl.BlockSpec((E_PAD, d), lambda i: (0, 0))
    return pl.pallas_call(
        functools.partial(_router_kernel, tm=tm), grid=(s // tm,),
        in_specs=[pl.BlockSpec((tm, d), lambda i: (i, 0)), row, row, wsp, wsp],
        out_specs=[pl.BlockSpec((8, tm), lambda i: (0, i)),
                   pl.BlockSpec((1, E_PAD, LANE), lambda i: (i, 0, 0)),
                   pl.BlockSpec((E_PAD, LANE), lambda i: (0, 0))],
        out_shape=[jax.ShapeDtypeStruct((8, s), F32), jax.ShapeDtypeStruct((s // tm, E_PAD, LANE), F32),
                   jax.ShapeDtypeStruct((E_PAD, LANE), F32)],
        scratch_shapes=[pltpu.VMEM((E_PAD, LANE), F32)],
        compiler_params=_cp("arbitrary"), name="moe_router",
    )(x, scale, shift, wt_hi, wt_lo)


def _gather_kernel(pb_ref, pt_ref, pv_ref, pf_ref, h_ref, s1_ref, s2_ref, o_ref):
    p = pl.program_id(0)

    @pl.when(pf_ref[p] == 1)
    def _():
        o_ref[...] = jnp.zeros_like(o_ref)

    @pl.when(pv_ref[p] == 1)
    def _():
        tb = MOE_BLOCK
        slot = lax.broadcasted_iota(I32, (tb, tb), 0) + pb_ref[p] * tb
        onehot = jnp.where(s1_ref[...] == slot, 1.0, jnp.where(s2_ref[...] == slot, 1.0, 0.0)).astype(BF16)
        o_ref[...] = (o_ref[...].astype(F32) + _dot(onehot, h_ref[...])).astype(BF16)


def moe_gather(h2, slot1_row, slot2_row, pb, pt, pv, pf, n_slots):
    s, d = h2.shape
    tb = MOE_BLOCK
    grid_spec = pltpu.PrefetchScalarGridSpec(
        num_scalar_prefetch=4, grid=(pb.shape[0],),
        in_specs=[pl.BlockSpec((tb, d), lambda p, pb, pt, pv, pf: (pt[p], 0)),
                  pl.BlockSpec((1, tb), lambda p, pb, pt, pv, pf: (0, pt[p])),
                  pl.BlockSpec((1, tb), lambda p, pb, pt, pv, pf: (0, pt[p]))],
        out_specs=pl.BlockSpec((tb, d), lambda p, pb, pt, pv, pf: (pb[p], 0)))
    return pl.pallas_call(
        _gather_kernel, grid_spec=grid_spec,
        out_shape=jax.ShapeDtypeStruct((n_slots, d), BF16),
        compiler_params=_cp("arbitrary"), name="moe_gather",
    )(pb, pt, pv, pf, h2, slot1_row, slot2_row)


def _expert_kernel(be_ref, nu_ref, x_ref, w1_ref, w3_ref, w2_ref, o_ref, acc_ref):
    b = pl.program_id(0)
    j = pl.program_id(1)
    live = b < nu_ref[0]

    @pl.when(jnp.logical_and(live, j == 0))
    def _():
        acc_ref[...] = jnp.zeros_like(acc_ref)

    @pl.when(live)
    def _():
        xb = x_ref[...]
        a = _dot(xb, w1_ref[0])
        act = (a * _sigmoid(a) * _dot(xb, w3_ref[0])).astype(BF16)
        acc_ref[...] += _dot(act, w2_ref[0])

    @pl.when(jnp.logical_and(live, j == pl.num_programs(1) - 1))
    def _():
        o_ref[...] = acc_ref[...].astype(BF16)


def moe_experts(xb, w1, w3, w2, block_e, n_used, tj=512):
    n_slots, d = xb.shape
    tb = MOE_BLOCK
    ff = w1.shape[2]
    nj = ff // tj

    def blk(b, j, be, nu):
        return jnp.minimum(b, nu[0] - 1)

    def jj(b, j, be, nu):
        return jnp.where(b < nu[0], j, nj - 1)

    grid_spec = pltpu.PrefetchScalarGridSpec(
        num_scalar_prefetch=2, grid=(n_slots // tb, nj),
        in_specs=[pl.BlockSpec((tb, d), lambda b, j, be, nu: (blk(b, j, be, nu), 0)),
                  pl.BlockSpec((1, d, tj), lambda b, j, be, nu: (be[blk(b, j, be, nu)], 0, jj(b, j, be, nu))),
                  pl.BlockSpec((1, d, tj), lambda b, j, be, nu: (be[blk(b, j, be, nu)], 0, jj(b, j, be, nu))),
                  pl.BlockSpec((1, tj, d), lambda b, j, be, nu: (be[blk(b, j, be, nu)], jj(b, j, be, nu), 0))],
        out_specs=pl.BlockSpec((tb, d), lambda b, j, be, nu: (blk(b, j, be, nu), 0)),
        scratch_shapes=[pltpu.VMEM((tb, d), F32)])
    return pl.pallas_call(
        _expert_kernel, grid_spec=grid_spec,
        out_shape=jax.ShapeDtypeStruct((n_slots, d), BF16),
        compiler_params=_cp("arbitrary", "arbitrary"), name="moe_experts",
    )(block_e, n_used, xb, w1, w3, w2)


def _combine_kernel(cb_ref, ct_ref, cv_ref, cf_ref, cl_ref, y_ref, s1_ref, s2_ref, w1_ref, w2_ref, x_ref, g_ref,
                    o_ref, acc_ref):
    p = pl.program_id(0)

    @pl.when(cf_ref[p] == 1)
    def _():
        acc_ref[...] = jnp.zeros_like(acc_ref)

    @pl.when(cv_ref[p] == 1)
    def _():
        tb = MOE_BLOCK
        slot = lax.broadcasted_iota(I32, (tb, tb), 1) + cb_ref[p] * tb
        wmat = jnp.where(s1_ref[...] == slot, w1_ref[...], 0.0) + jnp.where(s2_ref[...] == slot, w2_ref[...], 0.0)
        acc_ref[...] += _dot(wmat.astype(BF16), y_ref[...])

    @pl.when(cl_ref[p] == 1)
    def _():
        o_ref[...] = x_ref[...] + g_ref[...] * acc_ref[...]


def moe_combine(yb, slot1_col, slot2_col, w1_col, w2_col, x, gate, cb, ct, cv, cf, cl):
    s, d = x.shape
    tb = MOE_BLOCK
    col = pl.BlockSpec((tb, 1), lambda p, cb, ct, cv, cf, cl: (ct[p], 0))
    tile = pl.BlockSpec((tb, d), lambda p, cb, ct, cv, cf, cl: (ct[p], 0))
    grid_spec = pltpu.PrefetchScalarGridSpec(
        num_scalar_prefetch=5, grid=(cb.shape[0],),
        in_specs=[pl.BlockSpec((tb, d), lambda p, cb, ct, cv, cf, cl: (cb[p], 0)), col, col, col, col, tile,
                  pl.BlockSpec((1, d), lambda p, cb, ct, cv, cf, cl: (0, 0))],
        out_specs=tile,
        scratch_shapes=[pltpu.VMEM((tb, d), F32)])
    return pl.pallas_call(
        _combine_kernel, grid_spec=grid_spec,
        out_shape=jax.ShapeDtypeStruct((s, d), F32),
        compiler_params=_cp("arbitrary"), name="moe_combine",
    )(cb, ct, cv, cf, cl, yb, slot1_col, slot2_col, w1_col, w2_col, x, gate)


def _compact(valid, fields, length):
    valid = valid.reshape(-1)
    pos = jnp.where(valid, jnp.cumsum(valid.astype(I32)) - 1, length)
    n_valid = jnp.sum(valid.astype(I32))
    keep = jnp.arange(length, dtype=I32) < n_valid
    last = jnp.maximum(n_valid - 1, 0)
    out = []
    for f in fields:
        c = jnp.zeros((length,), I32).at[pos].set(f.reshape(-1).astype(I32), mode="drop")
        out.append(jnp.where(keep, c, c[last]))
    return out, keep.astype(I32), n_valid


def moe_ffn(x, h2, gate, scale, shift, w_router, w1, w3, w2):
    s, d = x.shape
    tb = MOE_BLOCK
    nt = s // tb
    n_slots = -(-(s * 2) // tb) * tb + N_EXPERTS * tb
    n_blocks = n_slots // tb
    wt = jnp.zeros((E_PAD, d), F32).at[:N_EXPERTS].set(w_router.T)
    wt_hi = wt.astype(BF16)
    wt_lo = (wt - wt_hi.astype(F32)).astype(BF16)
    info, base, cnt = moe_router(x, scale, shift, wt_hi, wt_lo, tm=tb)

    idx1, idx2 = info[0].astype(I32), info[1].astype(I32)
    counts = cnt[:N_EXPERTS, 0].astype(I32)
    padded = (counts + tb - 1) // tb * tb
    pend = jnp.cumsum(padded)
    pstart = pend - padded
    slot1 = pstart[idx1] + info[2].astype(I32)
    slot2 = pstart[idx2] + info[3].astype(I32)
    block_e = jnp.minimum(jnp.searchsorted(pend, jnp.arange(n_blocks, dtype=I32) * tb, side="right"),
                          N_EXPERTS - 1).astype(I32)
    n_used = (pend[-1] // tb).astype(I32).reshape(1)

    tile_lo = base[:, :N_EXPERTS, 0].astype(I32)
    tile_hi = jnp.concatenate([tile_lo[1:], counts[None]], axis=0)
    lo = pstart[None] + tile_lo
    hi = pstart[None] + tile_hi
    b_first = lo // tb
    b_last = (hi - 1) // tb
    nonempty = hi > lo
    blocks = jnp.stack([b_first, b_first + 1], axis=-1)
    valid = jnp.stack([nonempty, nonempty & (b_last > b_first)], axis=-1)
    tiles = jnp.broadcast_to(jnp.arange(nt, dtype=I32)[:, None, None], blocks.shape)
    n_pairs = n_blocks + N_EXPERTS * nt

    et = lambda a: jnp.transpose(a, (1, 0, 2))
    (pb, pt), pv, _ = _compact(et(valid), (et(blocks), et(tiles)), n_pairs)
    pf = jnp.concatenate([jnp.ones((1,), I32), (pb[1:] != pb[:-1]).astype(I32)])
    xb = moe_gather(h2, slot1.reshape(1, s), slot2.reshape(1, s), pb, pt, pv, pf, n_slots)
    yb = moe_experts(xb, w1, w3, w2, block_e, n_used)
    (cb, ct), cv, n_valid = _compact(valid, (blocks, tiles), n_pairs)
    cf = jnp.concatenate([jnp.ones((1,), I32), (ct[1:] != ct[:-1]).astype(I32)])
    is_last = jnp.arange(n_pairs, dtype=I32) == n_valid - 1
    cl = (jnp.concatenate([(ct[1:] != ct[:-1]), jnp.ones((1,), bool)]) | is_last).astype(I32) * cv
    return moe_combine(yb, slot1.reshape(s, 1), slot2.reshape(s, 1), info[4].reshape(s, 1), info[5].reshape(s, 1),
                       x, gate, cb, ct, cv, cf, cl)


def _pack_in_proj(w):
    sec = lambda k: w[:, IN_OFFS[k]:IN_OFFS[k + 1]]
    a_in, z, xbc, dt, cq, ckv, k_rope, uv, gates = (sec(k) for k in range(len(IN_SPLITS)))
    d = w.shape[0]
    small = jnp.concatenate([uv, a_in, xbc, cq, jnp.zeros((d, P_CKV - P_CQ - MLA_Q_LORA), w.dtype), ckv, z],
                            axis=1).astype(BF16)
    misc = jnp.concatenate([dt, jnp.zeros((d, LANE - SSD_HEADS), w.dtype), k_rope,
                            jnp.zeros((d, LANE - MLA_ROPE), w.dtype)], axis=1).astype(BF16)
    return small, misc, gates.astype(BF16)


def _pack_mla(w_uq, w_ukv, q_gain, k_gain):
    wq = w_uq.reshape(MLA_Q_LORA, MLA_HEADS, MLA_QK)
    wq = jnp.pad(wq, ((0, 0), (0, 0), (0, MLA_PAD - MLA_QK))).reshape(MLA_Q_LORA, MLA_HEADS * MLA_PAD)
    wkv = w_ukv.reshape(MLA_KV_LORA, MLA_HEADS, MLA_NOPE + MLA_V)
    wk = wkv[:, :, :MLA_NOPE].reshape(MLA_KV_LORA, MLA_HEADS * MLA_NOPE)
    wv = wkv[:, :, MLA_NOPE:].reshape(MLA_KV_LORA, MLA_HEADS * MLA_V)
    pad_gain = lambda g: jnp.pad(g, (0, MLA_PAD - MLA_QK)).reshape(1, MLA_PAD)
    return wq.astype(BF16), wk.astype(BF16), wv.astype(BF16), pad_gain(q_gain), pad_gain(k_gain)


def _pad_lanes(v):
    return jnp.pad(v, (0, LANE - v.shape[0])).reshape(1, LANE)


def kernel(x, c, positions, w_ada, b_ada, w_in, w_pool, pool_scale, ssd_conv_w, ssd_conv_b, ssd_dt_bias, ssd_a_log, ssd_d, ssd_norm, mla_q_norm, mla_w_uq, mla_kv_norm, mla_w_ukv, mla_q_gain, mla_k_gain, sgu_ln_gain, sgu_ln_bias, sgu_w_s, sgu_b_s, w_branch, w_out, ffn_w1, ffn_w3, ffn_w2, moe_router, moe_w1, moe_w3, moe_w2):
    b, s, d = x.shape
    assert b == 1 and d == D_MODEL and s % MOE_BLOCK == 0
    xs = x.reshape(s, d)
    mod = ada_mod(c, w_ada, b_ada)
    mods = [[mod[l, k * d:(k + 1) * d].reshape(1, d) for k in range(6)] for l in range(DEPTH)]

    half = MLA_ROPE // 2
    inv_freq = ROPE_THETA ** (-jnp.arange(half, dtype=F32) * 2.0 / MLA_ROPE)
    inv_row = jnp.concatenate([inv_freq, inv_freq, jnp.zeros((LANE - MLA_ROPE,), F32)]).reshape(1, LANE)
    cos, sin = rope_table(positions.astype(F32).reshape(s, 1), inv_row)
    heads = jnp.arange(LANE, dtype=I32)[:, None]
    chans = jnp.arange(MIX_WIDTH, dtype=I32)[None, :] // SSD_HEADDIM
    expand_m = (heads == chans).astype(BF16)

    h = norm_mod(xs, mods[0][1], mods[0][0])
    for l in range(DEPTH):
        shift1, scale1, gate1, shift2, scale2, gate2 = mods[l]
        w_small, w_misc, w_gate = _pack_in_proj(w_in[l])
        proj = matmul(h, w_small, BF16, 512, 1024, "in_proj")
        misc = matmul(h, w_misc, F32, 512, 2 * LANE, "in_proj_misc")
        y_pool = pool_mixer(proj, w_pool[l].astype(BF16), pool_scale[l].reshape(1, MIX_WIDTH))
        y_ssd = ssd_mixer(proj, misc, ssd_conv_w[l], ssd_conv_b[l].reshape(1, SSD_CONV_DIM),
                          _pad_lanes(ssd_dt_bias[l]), _pad_lanes(ssd_a_log[l]),
                          jnp.repeat(ssd_d[l], SSD_HEADDIM).reshape(1, MIX_WIDTH),
                          ssd_norm[l].reshape(1, MIX_WIDTH), expand_m)
        wq, wk, wv, qg, kg = _pack_mla(mla_w_uq[l], mla_w_ukv[l], mla_q_gain[l], mla_k_gain[l])
        q, k, vt = mla_prep(proj, misc, cos, sin, mla_q_norm[l].reshape(1, MLA_Q_LORA),
                            mla_kv_norm[l].reshape(1, MLA_KV_LORA), wq, wk, wv, qg, kg)
        y_att = flash_attention(q, k, vt)
        y_sgu = sgu_mixer(proj, sgu_ln_gain[l].reshape(1, MIX_WIDTH), sgu_ln_bias[l].reshape(1, MIX_WIDTH),
                          sgu_w_s[l], sgu_b_s[l].T)
        merged = merge_branches(h, (y_pool, y_ssd, y_att, y_sgu), w_gate, w_branch[l].astype(BF16))
        xs, h2 = resid_out(merged, w_out[l].astype(BF16), xs, gate1, scale2, shift2)
        idx = l // 2
        if l % 2 == 0:
            nxt = mods[l + 1] if l + 1 < DEPTH else mods[l]
            xs, h = dense_ffn(h2, ffn_w1[idx].astype(BF16), ffn_w3[idx].astype(BF16), ffn_w2[idx].astype(BF16),
                              xs, gate2, nxt[1], nxt[0])
        else:
            xs = moe_ffn(xs, h2, gate2, scale2, shift2, moe_router[idx], moe_w1[idx].astype(BF16),
                         moe_w3[idx].astype(BF16), moe_w2[idx].astype(BF16))
            if l + 1 < DEPTH:
                h = norm_mod(xs, mods[l + 1][1], mods[l + 1][0])
    return xs.reshape(b, s, d)
```
